```python
import math
import jax, jax.numpy as jnp
from jax import lax
import numpy as np

D_MODEL = 1024
BATCH = 8
SEQ = 2048
DEPTH = 1
DEC_BATCH = 32
DEC_SEQ = 8
PAST_LEN = 8192
PAGE_SIZE = 128

HEAD_DIM = 64
MIX_WIDTH = D_MODEL
SB_WIDTH = MIX_WIDTH // 2
DIFF_WIDTH = MIX_WIDTH - SB_WIDTH
SB_HEADS = SB_WIDTH // HEAD_DIM
DIFF_HEADS = DIFF_WIDTH // (2 * HEAD_DIM)
PROJ_WIDTH = 3 * SB_WIDTH + 3 * DIFF_WIDTH
N_EXPERTS = 32
TOP_K = 4
D_FF = D_MODEL
SWIGLU_LIMIT = 7.0
SWIGLU_ALPHA = 1.702
Q_BLOCK = 128
EXPERT_BLOCK = 128
LN_EPS = 1e-5
DEEPNORM_ALPHA = (2.0 * DEPTH) ** 0.25
DEEPNORM_BETA = (8.0 * DEPTH) ** -0.25
NEG_INF = -1e30

kernel_name = "hybrid_sb_diff_moe_deepnorm_step"


def layer_norm(x, g, b):
    xf = x.astype(jnp.float32)
    mu = jnp.mean(xf, axis=-1, keepdims=True)
    var = jnp.mean(jnp.square(xf - mu), axis=-1, keepdims=True)
    return ((xf - mu) * lax.rsqrt(var + LN_EPS) * g.astype(jnp.float32) + b.astype(jnp.float32)).astype(x.dtype)


def rms_norm(x, g):
    xf = x.astype(jnp.float32)
    ms = jnp.mean(jnp.square(xf), axis=-1, keepdims=True)
    return (xf * lax.rsqrt(ms + LN_EPS) * g.astype(jnp.float32)).astype(x.dtype)


def alibi_slopes(n_heads):
    h = jnp.arange(1, n_heads + 1, dtype=jnp.float32)
    return jnp.exp2(-8.0 * h / n_heads)


def query_blocks(q, q_pos):
    B, T = q.shape[0], q.shape[1]
    qb = min(Q_BLOCK, T)
    nb = -(-T // qb)
    pad = nb * qb - T
    q = jnp.pad(q, [(0, 0), (0, pad)] + [(0, 0)] * (q.ndim - 2))
    q_pos = jnp.pad(q_pos, (0, pad), mode="edge")
    q = jnp.moveaxis(q.reshape((B, nb, qb) + q.shape[2:]), 1, 0)
    return q, q_pos.reshape(nb, qb), T


def merge_blocks(o, T):
    o = jnp.moveaxis(o, 0, 1)
    o = o.reshape((o.shape[0], o.shape[1] * o.shape[2]) + o.shape[3:])
    return o[:, :T]


def stick_breaking_attention(q, k, v, q_pos, k_pos):
    scale = HEAD_DIM ** -0.5
    qblk, pblk, T = query_blocks(q, q_pos)

    def one_block(args):
        qb, pb = args
        z = jnp.einsum("bqhd,bkhd->bhqk", qb, k, preferred_element_type=jnp.float32) * scale
        mask = k_pos[None, :] < pb[:, None]
        log_1m = jnp.where(mask, jax.nn.log_sigmoid(-z), 0.0)
        log_rest = lax.cumsum(log_1m, axis=3, reverse=True) - log_1m
        w = jnp.where(mask, jnp.exp(jax.nn.log_sigmoid(z) + log_rest), 0.0)
        return jnp.einsum("bhqk,bkhd->bqhd", w.astype(v.dtype), v)

    return merge_blocks(lax.map(one_block, (qblk, pblk)), T)


def differential_attention(q, k, v, q_pos, k_pos, lam, slopes):
    scale = HEAD_DIM ** -0.5
    qblk, pblk, T = query_blocks(q, q_pos)

    def one_block(args):
        qb, pb = args
        z = jnp.einsum("bqhcd,bkhcd->bchqk", qb, k, preferred_element_type=jnp.float32) * scale
        dist = (pb[:, None] - k_pos[None, :]).astype(jnp.float32)
        z = z - slopes[:, None, None] * dist[None]
        z = jnp.where(dist >= 0.0, z, NEG_INF)
        p = jax.nn.softmax(z, axis=-1)
        a = p[:, 0] - lam * p[:, 1]
        return jnp.einsum("bhqk,bkhe->bqhe", a.astype(v.dtype), v)

    return merge_blocks(lax.map(one_block, (qblk, pblk)), T)


def token_mixer(h, past, w_in, lam_q1, lam_k1, lam_q2, lam_k2, subln_g, w_out, lambda_init):
    B, T, _ = h.shape
    proj = jnp.einsum("btd,de->bte", h, w_in)
    splits = [SB_WIDTH, 2 * SB_WIDTH, 3 * SB_WIDTH, 3 * SB_WIDTH + DIFF_WIDTH, 3 * SB_WIDTH + 2 * DIFF_WIDTH]
    sb_q, sb_k, sb_v, d_q, d_k, d_v = jnp.split(proj, splits, axis=-1)
    sb_q = sb_q.reshape(B, T, SB_HEADS, HEAD_DIM)
    sb_k = sb_k.reshape(B, T, SB_HEADS, HEAD_DIM)
    sb_v = sb_v.reshape(B, T, SB_HEADS, HEAD_DIM)
    d_q = d_q.reshape(B, T, DIFF_HEADS, 2, HEAD_DIM)
    d_k = d_k.reshape(B, T, DIFF_HEADS, 2, HEAD_DIM)
    d_v = d_v.reshape(B, T, DIFF_HEADS, 2 * HEAD_DIM)
    new_rows = (sb_k, sb_v, d_k, d_v)
    if past is None:
        P = 0
        keys = new_rows
    else:
        P = past[0].shape[1]
        keys = (jnp.concatenate([past[0], sb_k], axis=1), jnp.concatenate([past[1], sb_v], axis=1),
                jnp.concatenate([past[2], d_k], axis=1), jnp.concatenate([past[3], d_v], axis=1))
    q_pos = P + jnp.arange(T, dtype=jnp.int32)
    k_pos = jnp.arange(P + T, dtype=jnp.int32)

    sb_out = stick_breaking_attention(sb_q, keys[0], keys[1], q_pos, k_pos)
    lam = (jnp.exp(jnp.sum(lam_q1.astype(jnp.float32) * lam_k1.astype(jnp.float32)))
           - jnp.exp(jnp.sum(lam_q2.astype(jnp.float32) * lam_k2.astype(jnp.float32))) + lambda_init)
    d_out = differential_attention(d_q, keys[2], keys[3], q_pos, k_pos, lam, alibi_slopes(DIFF_HEADS))
    d_out = rms_norm(d_out, subln_g) * (1.0 - lambda_init)
    mixed = jnp.concatenate([sb_out.reshape(B, T, SB_WIDTH), d_out.reshape(B, T, DIFF_WIDTH)], axis=-1)
    return jnp.einsum("btm,md->btd", mixed, w_out), new_rows


def moe_ffn(x2d, w_router, b_router, w_gate_up, b_gate_up, w_down, b_down):
    n = x2d.shape[0]
    logits = jnp.einsum("nd,de->ne", x2d, w_router, preferred_element_type=jnp.float32) + b_router.astype(jnp.float32)
    top_vals, top_idx = lax.top_k(logits, TOP_K)
    gates = jax.nn.softmax(top_vals, axis=-1)
    flat_e = top_idx.reshape(-1)
    flat_tok = jnp.repeat(jnp.arange(n, dtype=jnp.int32), TOP_K)
    flat_g = gates.reshape(-1)
    order = jnp.argsort(flat_e)
    se, stok, sg = flat_e[order], flat_tok[order], flat_g[order]
    counts = jnp.bincount(flat_e, length=N_EXPERTS)
    padded = ((counts + EXPERT_BLOCK - 1) // EXPERT_BLOCK) * EXPERT_BLOCK
    pend = jnp.cumsum(padded)
    pstart = pend - padded
    ustart = jnp.cumsum(counts) - counts
    dest = pstart[se] + (jnp.arange(n * TOP_K, dtype=jnp.int32) - ustart[se])
    n_rows = -(-(n * TOP_K + N_EXPERTS * (EXPERT_BLOCK - 1)) // EXPERT_BLOCK) * EXPERT_BLOCK
    n_blocks = n_rows // EXPERT_BLOCK
    row_tok = jnp.zeros((n_rows,), jnp.int32).at[dest].set(stok)
    row_gate = jnp.zeros((n_rows,), jnp.float32).at[dest].set(sg)
    block_start = jnp.arange(n_blocks, dtype=jnp.int32) * EXPERT_BLOCK
    block_expert = jnp.clip(jnp.searchsorted(pend, block_start, side="right"), 0, N_EXPERTS - 1)

    def expert_block(args):
        toks, e = args
        xb = x2d[toks]
        gu = xb @ w_gate_up[e] + b_gate_up[e]
        g, u = gu[:, :D_FF], gu[:, D_FF:]
        g = jnp.minimum(g, SWIGLU_LIMIT)
        u = jnp.clip(u, -SWIGLU_LIMIT, SWIGLU_LIMIT)
        hid = (u + 1.0) * (g * jax.nn.sigmoid(SWIGLU_ALPHA * g))
        return hid @ w_down[e] + b_down[e]

    out = lax.map(expert_block, (row_tok.reshape(n_blocks, EXPERT_BLOCK), block_expert))
    out = out.reshape(n_rows, -1).astype(jnp.float32) * row_gate[:, None]
    y = jnp.zeros((n, x2d.shape[1]), jnp.float32).at[row_tok].add(out)
    return y.astype(x2d.dtype)


def decoder_layer(x, past, w_in, lam_q1, lam_k1, lam_q2, lam_k2, subln_g, w_out, ln1_g, ln1_b,
                  w_router, b_router, w_gate_up, b_gate_up, w_down, b_down, ln2_g, ln2_b, lambda_init):
    mix_out, new_rows = token_mixer(x, past, w_in, lam_q1, lam_k1, lam_q2, lam_k2, subln_g, w_out, lambda_init)
    x = layer_norm(DEEPNORM_ALPHA * x + mix_out, ln1_g, ln1_b)
    B, T, D = x.shape
    ffn = moe_ffn(x.reshape(B * T, D), w_router, b_router, w_gate_up, b_gate_up, w_down, b_down).reshape(B, T, D)
    x = layer_norm(DEEPNORM_ALPHA * x + ffn, ln2_g, ln2_b)
    return x, new_rows


def gather_pages(pool, layer, page_table):
    g = pool[layer, page_table]
    return g.reshape((g.shape[0], g.shape[1] * g.shape[2]) + g.shape[3:])


def setup_inputs(seed: int = 0) -> dict:
    key = jax.random.key(seed)
    ks = jax.random.split(key, 24)
    n_pages = PAST_LEN // PAGE_SIZE
    n_used = DEC_BATCH * n_pages
    n_pool = n_used + -(-n_used // 4)
    nrm = jax.random.normal
    x_prompt = nrm(ks[0], (BATCH, SEQ, D_MODEL), jnp.float32)
    x_sample = nrm(ks[1], (DEC_BATCH, DEC_SEQ, D_MODEL), jnp.float32)
    cache_sb_k = nrm(ks[2], (DEPTH, n_pool, PAGE_SIZE, SB_HEADS, HEAD_DIM), jnp.float32)
    cache_sb_v = nrm(ks[3], (DEPTH, n_pool, PAGE_SIZE, SB_HEADS, HEAD_DIM), jnp.float32) * DEEPNORM_BETA
    cache_diff_k = nrm(ks[4], (DEPTH, n_pool, PAGE_SIZE, DIFF_HEADS, 2, HEAD_DIM), jnp.float32)
    cache_diff_v = nrm(ks[5], (DEPTH, n_pool, PAGE_SIZE, DIFF_HEADS, 2 * HEAD_DIM), jnp.float32) * DEEPNORM_BETA
    page_table = jax.random.permutation(ks[6], n_pool)[:n_used].reshape(DEC_BATCH, n_pages).astype(jnp.int32)
    col_scale = jnp.concatenate([jnp.ones((2 * SB_WIDTH,), jnp.float32), jnp.full((SB_WIDTH,), DEEPNORM_BETA, jnp.float32),
                                 jnp.ones((2 * DIFF_WIDTH,), jnp.float32), jnp.full((DIFF_WIDTH,), DEEPNORM_BETA, jnp.float32)])
    w_in = nrm(ks[7], (DEPTH, D_MODEL, PROJ_WIDTH), jnp.float32) * D_MODEL ** -0.5 * col_scale
    lambda_q1 = 0.1 * nrm(ks[8], (DEPTH, HEAD_DIM), jnp.float32)
    lambda_k1 = 0.1 * nrm(ks[9], (DEPTH, HEAD_DIM), jnp.float32)
    lambda_q2 = 0.1 * nrm(ks[10], (DEPTH, HEAD_DIM), jnp.float32)
    lambda_k2 = 0.1 * nrm(ks[11], (DEPTH, HEAD_DIM), jnp.float32)
    subln_g = 1.0 + 0.1 * nrm(ks[12], (DEPTH, 2 * HEAD_DIM), jnp.float32)
    w_out = nrm(ks[13], (DEPTH, MIX_WIDTH, D_MODEL), jnp.float32) * MIX_WIDTH ** -0.5 * DEEPNORM_BETA
    ln1_g = 1.0 + 0.1 * nrm(ks[14], (DEPTH, D_MODEL), jnp.float32)
    ln1_b = 0.02 * nrm(ks[15], (DEPTH, D_MODEL), jnp.float32)
    w_router = nrm(ks[16], (DEPTH, D_MODEL, N_EXPERTS), jnp.float32) * D_MODEL ** -0.5
    b_router = 0.01 * nrm(ks[17], (DEPTH, N_EXPERTS), jnp.float32)
    w_gate_up = nrm(ks[18], (DEPTH, N_EXPERTS, D_MODEL, 2 * D_FF), jnp.float32) * D_MODEL ** -0.5 * DEEPNORM_BETA
    b_gate_up = 0.01 * nrm(ks[19], (DEPTH, N_EXPERTS, 2 * D_FF), jnp.float32)
    w_down = nrm(ks[20], (DEPTH, N_EXPERTS, D_FF, D_MODEL), jnp.float32) * D_FF ** -0.5 * DEEPNORM_BETA
    b_down = 0.01 * nrm(ks[21], (DEPTH, N_EXPERTS, D_MODEL), jnp.float32)
    ln2_g = 1.0 + 0.1 * nrm(ks[22], (DEPTH, D_MODEL), jnp.float32)
    ln2_b = 0.02 * nrm(ks[23], (DEPTH, D_MODEL), jnp.float32)
    return {"x_prompt": x_prompt, "x_sample": x_sample,
            "cache_sb_k": cache_sb_k, "cache_sb_v": cache_sb_v,
            "cache_diff_k": cache_diff_k, "cache_diff_v": cache_diff_v,
            "page_table": page_table, "w_in": w_in,
            "lambda_q1": lambda_q1, "lambda_k1": lambda_k1, "lambda_q2": lambda_q2, "lambda_k2": lambda_k2,
            "subln_g": subln_g, "w_out": w_out, "ln1_g": ln1_g, "ln1_b": ln1_b,
            "w_router": w_router, "b_router": b_router, "w_gate_up": w_gate_up, "b_gate_up": b_gate_up,
            "w_down": w_down, "b_down": b_down, "ln2_g": ln2_g, "ln2_b": ln2_b}


def reference(x_prompt, x_sample, cache_sb_k, cache_sb_v, cache_diff_k, cache_diff_v, page_table, w_in,
              lambda_q1, lambda_k1, lambda_q2, lambda_k2, subln_g, w_out, ln1_g, ln1_b,
              w_router, b_router, w_gate_up, b_gate_up, w_down, b_down, ln2_g, ln2_b):
    yp, ys = x_prompt, x_sample
    p_sbk, p_sbv, p_dk, p_dv = [], [], [], []
    s_sbk, s_sbv, s_dk, s_dv = [], [], [], []
    for i in range(DEPTH):
        lambda_init = 0.8 - 0.6 * math.exp(-0.3 * i)
        lw = (w_in[i], lambda_q1[i], lambda_k1[i], lambda_q2[i], lambda_k2[i], subln_g[i], w_out[i], ln1_g[i], ln1_b[i],
              w_router[i], b_router[i], w_gate_up[i], b_gate_up[i], w_down[i], b_down[i], ln2_g[i], ln2_b[i])
        past = (gather_pages(cache_sb_k, i, page_table), gather_pages(cache_sb_v, i, page_table),
                gather_pages(cache_diff_k, i, page_table), gather_pages(cache_diff_v, i, page_table))
        yp, rows_p = decoder_layer(yp, None, *lw, lambda_init)
        ys, rows_s = decoder_layer(ys, past, *lw, lambda_init)
        p_sbk.append(rows_p[0]); p_sbv.append(rows_p[1]); p_dk.append(rows_p[2]); p_dv.append(rows_p[3])
        s_sbk.append(rows_s[0]); s_sbv.append(rows_s[1]); s_dk.append(rows_s[2]); s_dv.append(rows_s[3])
    return (yp, ys,
            jnp.stack(p_sbk), jnp.stack(p_sbv), jnp.stack(p_dk), jnp.stack(p_dv),
            jnp.stack(s_sbk), jnp.stack(s_sbv), jnp.stack(s_dk), jnp.stack(s_dv))
```

```python
import functools
import math

import jax
import jax.numpy as jnp
from jax import lax
from jax.experimental import pallas as pl
from jax.experimental.pallas import tpu as pltpu

F32 = jnp.float32
BF16 = jnp.bfloat16

HEAD_DIM = 64
LANES = 128
TOP_K = 4
SWIGLU_LIMIT = 7.0
SWIGLU_ALPHA = 1.702
LN_EPS = 1e-5
NEG_INF = -1e30
EXPERT_ROWS = 256
VMEM_LIMIT = 56 * 1024 * 1024


def _cparams(sem):
    return pltpu.CompilerParams(dimension_semantics=sem, vmem_limit_bytes=VMEM_LIMIT)


def _in_proj_kernel(x_ref, w_ref, sbk_ref, sbv_ref, dk_ref, dv_ref, pb_ref, *, width, scale):
    x = x_ref[...].astype(BF16)
    f32_outs = {1: sbk_ref, 2: sbv_ref, 4: dk_ref, 5: dv_ref}
    for c in range(6):
        cols = slice(c * width, (c + 1) * width)
        p = jnp.dot(x, w_ref[:, cols], preferred_element_type=F32)
        if c in f32_outs:
            f32_outs[c][...] = p
        else:
            p = p * scale
        pb_ref[:, cols] = p.astype(pb_ref.dtype)


def _in_proj(x2d, w_bf16, tm, pb_dtype):
    n, d = x2d.shape
    width = w_bf16.shape[1] // 6
    kern = functools.partial(_in_proj_kernel, width=width, scale=HEAD_DIM ** -0.5)
    out_f32 = jax.ShapeDtypeStruct((n, width), F32)
    row_spec = pl.BlockSpec((tm, width), lambda i: (i, 0))
    return pl.pallas_call(
        kern,
        grid=(n // tm,),
        in_specs=[pl.BlockSpec((tm, d), lambda i: (i, 0)),
                  pl.BlockSpec(w_bf16.shape, lambda i: (0, 0))],
        out_specs=[row_spec] * 4 + [pl.BlockSpec((tm, 6 * width), lambda i: (i, 0))],
        out_shape=[out_f32] * 4 + [jax.ShapeDtypeStruct((n, 6 * width), pb_dtype)],
        compiler_params=_cparams(("parallel",)),
        name="in_proj",
    )(x2d, w_bf16)


def _in_proj_t_kernel(x_ref, wn_ref, wt_ref, q_ref, sbk_ref, sbv_ref, dk_ref, dv_ref, kvt_ref, dvb_ref, *,
                      width, scale):
    x = x_ref[...].astype(BF16)
    for c in range(2):
        cols = slice(c * width, (c + 1) * width)
        q = jnp.dot(x, wn_ref[:, cols], preferred_element_type=F32) * scale
        q_ref[:, cols] = q.astype(BF16)
    dv = jnp.dot(x, wn_ref[:, 2 * width:], preferred_element_type=F32)
    dv_ref[...] = dv
    dvb_ref[...] = dv.astype(BF16)
    for c, out in enumerate((sbk_ref, sbv_ref, dk_ref)):
        rows = slice(c * width, (c + 1) * width)
        pt = _qk(wt_ref[rows, :], x)
        out[...] = pt
        kvt_ref[rows, :] = pt.astype(BF16)


def _in_proj_t(x3d, w_nat, w_t, tm):
    batch, seq, d = x3d.shape
    width = w_t.shape[0] // 3
    nt = seq // tm
    kern = functools.partial(_in_proj_t_kernel, width=width, scale=HEAD_DIM ** -0.5)
    t_spec = pl.BlockSpec((None, width, tm), lambda b, i: (b, 0, i))
    t_shape = jax.ShapeDtypeStruct((batch, width, seq), F32)
    return pl.pallas_call(
        kern,
        grid=(batch, nt),
        in_specs=[pl.BlockSpec((None, tm, d), lambda b, i: (b, i, 0)),
                  pl.BlockSpec(w_nat.shape, lambda b, i: (0, 0)),
                  pl.BlockSpec(w_t.shape, lambda b, i: (0, 0))],
        out_specs=[pl.BlockSpec((tm, 2 * width), lambda b, i: (b * nt + i, 0)), t_spec, t_spec, t_spec,
                   pl.BlockSpec((tm, width), lambda b, i: (b * nt + i, 0)),
                   pl.BlockSpec((None, 3 * width, tm), lambda b, i: (b, 0, i)),
                   pl.BlockSpec((tm, width), lambda b, i: (b * nt + i, 0))],
        out_shape=[jax.ShapeDtypeStruct((batch * seq, 2 * width), BF16), t_shape, t_shape, t_shape,
                   jax.ShapeDtypeStruct((batch * seq, width), F32),
                   jax.ShapeDtypeStruct((batch, 3 * width, seq), BF16),
                   jax.ShapeDtypeStruct((batch * seq, width), BF16)],
        compiler_params=_cparams(("parallel", "parallel")),
        name="in_proj_t",
    )(x3d, w_nat, w_t)


def _softplus(z):
    return jnp.maximum(z, 0.0) + jnp.log1p(jnp.exp(-jnp.abs(z)))


def _suffix_sum_matrix():
    r = lax.broadcasted_iota(jnp.int32, (LANES, 2 * LANES), 0)
    c = lax.broadcasted_iota(jnp.int32, (LANES, 2 * LANES), 1)
    tt = jnp.where((r > c) | (c >= LANES), 1.0, 0.0).astype(BF16)
    return jnp.concatenate([tt, tt], axis=0)


def _stick_block(z, mask, run, tt2):
    sp = _softplus(z)
    log_1m = -sp if mask is None else jnp.where(mask, -sp, 0.0)
    hi = log_1m.astype(BF16)
    lo = (log_1m - hi.astype(F32)).astype(BF16)
    ct = jnp.dot(jnp.concatenate([hi, lo], axis=1), tt2, preferred_element_type=F32)
    log_rest = ct[:, :LANES] + run
    w = jnp.exp(z - sp + log_rest)
    if mask is not None:
        w = jnp.where(mask, w, 0.0)
    return w, run + ct[:, LANES:]


def _qk(q, k):
    return lax.dot_general(q, k, (((1,), (1,)), ((), ())), preferred_element_type=F32)


def _split_heads_rows(q):
    lane = lax.broadcasted_iota(jnp.int32, q.shape, 1)
    zero = jnp.zeros_like(q)
    return jnp.concatenate([jnp.where(lane < HEAD_DIM, q, zero), jnp.where(lane >= HEAD_DIM, q, zero)], axis=0)


def _sb_prompt_kernel(q_ref, k_ref, v_ref, o_ref, acc_ref, run_ref):
    i = pl.program_id(2)
    tq = q_ref.shape[0]
    q2 = _split_heads_rows(q_ref[...])
    tt2 = _suffix_sum_matrix()
    row = lax.broadcasted_iota(jnp.int32, (2 * tq, LANES), 0) & (tq - 1)
    col = lax.broadcasted_iota(jnp.int32, (2 * tq, LANES), 1)
    causal = col < row
    acc_ref[...] = jnp.zeros_like(acc_ref)
    run_ref[...] = jnp.zeros_like(run_ref)

    def body(jj, carry):
        j = i - jj
        start = pl.multiple_of(j * LANES, LANES)
        kb_t = k_ref[:, pl.ds(start, LANES)]
        vb_t = v_ref[:, pl.ds(start, LANES)]
        z = jnp.dot(q2, kb_t, preferred_element_type=F32)
        mask = causal | (jj > 0)
        w, run = _stick_block(z, mask, run_ref[...], tt2)
        run_ref[...] = run
        acc_ref[...] += _qk(w.astype(BF16), vb_t)
        return carry

    lax.fori_loop(0, i + 1, body, 0)
    acc = acc_ref[...]
    lane = lax.broadcasted_iota(jnp.int32, (tq, LANES), 1)
    o_ref[...] = jnp.where(lane < HEAD_DIM, acc[:tq], acc[tq:]).astype(o_ref.dtype)


def _sb_prompt(q_b, kv_t):
    batch, _, seq = kv_t.shape
    tq = LANES
    nq = seq // tq
    n_pairs = 4
    return pl.pallas_call(
        _sb_prompt_kernel,
        grid=(batch, n_pairs, nq),
        in_specs=[pl.BlockSpec((tq, LANES), lambda b, g, i: (b * nq + i, g)),
                  pl.BlockSpec((None, LANES, seq), lambda b, g, i: (b, g, 0)),
                  pl.BlockSpec((None, LANES, seq), lambda b, g, i: (b, 4 + g, 0))],
        out_specs=pl.BlockSpec((tq, LANES), lambda b, g, i: (b * nq + i, g)),
        out_shape=jax.ShapeDtypeStruct((batch * seq, n_pairs * LANES), BF16),
        scratch_shapes=[pltpu.VMEM((2 * tq, LANES), F32), pltpu.VMEM((2 * tq, LANES), F32)],
        compiler_params=_cparams(("parallel", "parallel", "arbitrary")),
        name="sb_prompt",
    )(q_b, kv_t, kv_t)


def _lambda_value(lam_ref, lambda_init):
    lp = lam_ref[...]
    a = jnp.sum(lp[0:1, :] * lp[1:2, :], axis=1, keepdims=True)
    b = jnp.sum(lp[2:3, :] * lp[3:4, :], axis=1, keepdims=True)
    return jnp.exp(a) - jnp.exp(b) + lambda_init


def _subln(a, g, lambda_init):
    ms = jnp.mean(a * a, axis=1, keepdims=True)
    return (a * lax.rsqrt(ms + LN_EPS) * g) * (1.0 - lambda_init)


def _diff_prompt_kernel(slopes_ref, q_ref, k_ref, v_ref, lam_ref, g_ref, o_ref, acc_ref, m_ref, l_ref, *,
                        lambda_init):
    h = pl.program_id(1)
    i = pl.program_id(2)
    tq = q_ref.shape[0]
    q2 = _split_heads_rows(q_ref[...])
    slope = slopes_ref[h]
    row = lax.broadcasted_iota(jnp.int32, (2 * tq, LANES), 0) & (tq - 1)
    col = lax.broadcasted_iota(jnp.int32, (2 * tq, LANES), 1)
    rel = (row - col).astype(F32)
    acc_ref[...] = jnp.zeros_like(acc_ref)
    m_ref[...] = jnp.full_like(m_ref, NEG_INF)
    l_ref[...] = jnp.zeros_like(l_ref)

    def body(jj, carry):
        j = i - jj
        start = pl.multiple_of(j * LANES, LANES)
        kb_t = k_ref[:, pl.ds(start, LANES)]
        vb = v_ref[pl.ds(start, LANES), :]
        dist = rel + (jj * LANES).astype(F32)
        z = jnp.dot(q2, kb_t, preferred_element_type=F32) - slope * dist
        z = jnp.where(dist >= 0.0, z, NEG_INF)
        m_prev = m_ref[...]
        m_new = jnp.maximum(m_prev, jnp.max(z, axis=1, keepdims=True))
        alpha = jnp.exp(m_prev - m_new)
        p = jnp.exp(z - m_new)
        l_ref[...] = alpha * l_ref[...] + jnp.sum(p, axis=1, keepdims=True)
        acc_ref[...] = alpha * acc_ref[...] + jnp.dot(p.astype(BF16), vb, preferred_element_type=F32)
        m_ref[...] = m_new
        return carry

    lax.fori_loop(0, i + 1, body, 0)
    o = acc_ref[...] / l_ref[...]
    a = o[:tq] - _lambda_value(lam_ref, lambda_init) * o[tq:]
    o_ref[...] = _subln(a, g_ref[...], lambda_init).astype(o_ref.dtype)


def _diff_prompt(q_b, kv_t, dv_b, slopes, lam_params, subln_g, lambda_init):
    batch, _, seq = kv_t.shape
    tq = LANES
    nq = seq // tq
    n_heads = 4
    kern = functools.partial(_diff_prompt_kernel, lambda_init=lambda_init)
    return pl.pallas_call(
        kern,
        grid=(batch, n_heads, nq),
        in_specs=[pl.BlockSpec(memory_space=pltpu.SMEM),
                  pl.BlockSpec((tq, LANES), lambda b, h, i: (b * nq + i, 4 + h)),
                  pl.BlockSpec((None, LANES, seq), lambda b, h, i: (b, 8 + h, 0)),
                  pl.BlockSpec((seq, LANES), lambda b, h, i: (b, h)),
                  pl.BlockSpec(lam_params.shape, lambda b, h, i: (0, 0)),
                  pl.BlockSpec(subln_g.shape, lambda b, h, i: (0, 0))],
        out_specs=pl.BlockSpec((tq, LANES), lambda b, h, i: (b * nq + i, h)),
        out_shape=jax.ShapeDtypeStruct((batch * seq, n_heads * LANES), BF16),
        scratch_shapes=[pltpu.VMEM((2 * tq, LANES), F32)] * 3,
        compiler_params=_cparams(("parallel", "parallel", "arbitrary")),
        name="diff_prompt",
    )(slopes, q_b, kv_t, dv_b, lam_params, subln_g)


def _pad_rows(x, rows):
    return jnp.concatenate([x, jnp.zeros((rows - x.shape[0], x.shape[1]), x.dtype)], axis=0)


def _block_diag_queries(q, group_lanes, n_groups):
    t = q.shape[0]
    qt = jnp.concatenate([q] * n_groups, axis=0)
    rg = lax.broadcasted_iota(jnp.int32, qt.shape, 0) // t
    lg = lax.broadcasted_iota(jnp.int32, qt.shape, 1) // group_lanes
    return jnp.where(rg == lg, qt, 0.0)


def _sb_decode_kernel(pt_ref, q_ref, kn_ref, vn_ref, *rest, pages_per_step, n_steps):
    del pt_ref
    pp = pages_per_step
    k_refs, v_refs = rest[:pp], rest[pp:2 * pp]
    o_ref, acc_ref, run_ref, q_s = rest[2 * pp:]
    s = pl.program_id(1)
    t = q_ref.shape[0]
    n_heads = q_ref.shape[1] // HEAD_DIM
    rows = n_heads * t
    tt2 = _suffix_sum_matrix()

    @pl.when(s == 0)
    def _():
        qbd = _block_diag_queries(q_ref[...], HEAD_DIM, n_heads).astype(BF16)
        q_s[...] = qbd
        tok = lax.broadcasted_iota(jnp.int32, (rows, LANES), 0) % t
        col = lax.broadcasted_iota(jnp.int32, (rows, LANES), 1)
        kb = _pad_rows(kn_ref[...], LANES).astype(BF16)
        vb = _pad_rows(vn_ref[...], LANES).astype(BF16)
        w, run = _stick_block(_qk(qbd, kb), col < tok, jnp.zeros(run_ref.shape, F32), tt2)
        acc_ref[...] = jnp.dot(w.astype(BF16), vb, preferred_element_type=F32)
        run_ref[...] = run

    qbd = q_s[...]
    acc = acc_ref[...]
    run = run_ref[...]
    for p in range(pp):
        z = jnp.dot(qbd, k_refs[p][...].astype(BF16), preferred_element_type=F32)
        w, run = _stick_block(z, None, run, tt2)
        acc = acc + _qk(w.astype(BF16), v_refs[p][...].astype(BF16))
    acc_ref[...] = acc
    run_ref[...] = run

    @pl.when(s == n_steps - 1)
    def _():
        rg = lax.broadcasted_iota(jnp.int32, acc.shape, 0) // t
        lg = lax.broadcasted_iota(jnp.int32, acc.shape, 1) // HEAD_DIM
        own = jnp.where(rg == lg, acc, 0.0)
        out = own[0:t]
        for hh in range(1, n_heads):
            out = out + own[hh * t:(hh + 1) * t]
        o_ref[...] = out


def _diff_decode_kernel(pt_ref, q_ref, kn_ref, vn_ref, *rest, pages_per_step, n_steps, n_pages, lambda_init):
    del pt_ref
    pp = pages_per_step
    k_refs, v_refs = rest[:pp], rest[pp:2 * pp]
    lam_ref, g_ref, o_ref, acc_ref, m_ref, l_ref, q_s = rest[2 * pp:]
    s = pl.program_id(1)
    t = q_ref.shape[0]
    n_heads = q_ref.shape[1] // LANES
    hr = 2 * t
    rows = n_heads * hr
    page = k_refs[0].shape[1]
    r = lax.broadcasted_iota(jnp.int32, (rows, LANES), 0)
    col = lax.broadcasted_iota(jnp.int32, (rows, LANES), 1)
    tok = r % t
    slope = jnp.exp2(-8.0 * (r // hr + 1).astype(F32) / n_heads)

    def block(z, dist, mask, v_of_head, acc, m_prev, l_prev):
        z = z - slope * dist
        if mask is not None:
            z = jnp.where(mask, z, NEG_INF)
        m_new = jnp.maximum(m_prev, jnp.max(z, axis=1, keepdims=True))
        alpha = jnp.exp(m_prev - m_new)
        p = jnp.exp(z - m_new)
        l_new = alpha * l_prev + jnp.sum(p, axis=1, keepdims=True)
        p = p.astype(BF16)
        pv = [jnp.dot(p[hh * hr:(hh + 1) * hr], v_of_head(hh), preferred_element_type=F32) for hh in range(n_heads)]
        return alpha * acc + jnp.concatenate(pv, axis=0), m_new, l_new

    @pl.when(s == 0)
    def _():
        qbd = _block_diag_queries(q_ref[...], HEAD_DIM, 2 * n_heads).astype(BF16)
        q_s[...] = qbd
        kb = _pad_rows(kn_ref[...], LANES).astype(BF16)
        vb = _pad_rows(vn_ref[...], LANES).astype(BF16)
        acc, m_new, l_new = block(_qk(qbd, kb), (tok - col).astype(F32), col <= tok,
                                  lambda hh: vb[:, hh * LANES:(hh + 1) * LANES],
                                  jnp.zeros(acc_ref.shape, F32), jnp.full(m_ref.shape, NEG_INF, F32),
                                  jnp.zeros(l_ref.shape, F32))
        acc_ref[...] = acc
        m_ref[...] = m_new
        l_ref[...] = l_new

    qbd = q_s[...]
    acc = acc_ref[...]
    m_run = m_ref[...]
    l_run = l_ref[...]
    past_len = n_pages * page
    for p in range(pp):
        logical = n_pages - 1 - (s * pp + p)
        dist = (past_len + tok - logical * page - col).astype(F32)
        z = jnp.dot(qbd, k_refs[p][...].astype(BF16), preferred_element_type=F32)
        v_ref = v_refs[p]
        acc, m_run, l_run = block(z, dist, None, lambda hh: v_ref[:, hh, :].astype(BF16), acc, m_run, l_run)
    acc_ref[...] = acc
    m_ref[...] = m_run
    l_ref[...] = l_run

    @pl.when(s == n_steps - 1)
    def _():
        o = acc / l_run
        lam = _lambda_value(lam_ref, lambda_init)
        g = g_ref[...]
        pieces = []
        for hh in range(n_heads):
            a = o[hh * hr:hh * hr + t] - lam * o[hh * hr + t:(hh + 1) * hr]
            pieces.append(_subln(a, g, lambda_init))
        o_ref[...] = jnp.concatenate(pieces, axis=1)


def _decode_call(kern, pb, col0, cache_k, cache_v, page_table, extra_inputs, scratch, t, pages_per_step, name):
    batch, n_pages = page_table.shape
    width = pb.shape[1] // 6
    n_steps = n_pages // pages_per_step

    def new_spec(c):
        return pl.BlockSpec((t, width), lambda b, s, pt: (b, c))

    def page_specs(cache):
        zeros = (0,) * (cache.ndim - 1)

        def spec(p):
            return pl.BlockSpec((None,) + cache.shape[1:],
                                lambda b, s, pt: (pt[b, n_pages - 1 - (s * pages_per_step + p)],) + zeros)

        return [spec(p) for p in range(pages_per_step)]

    extra_specs = [pl.BlockSpec(x.shape, lambda b, s, pt: (0, 0)) for x in extra_inputs]
    grid_spec = pltpu.PrefetchScalarGridSpec(
        num_scalar_prefetch=1,
        grid=(batch, n_steps),
        in_specs=([new_spec(col0), new_spec(col0 + 1), new_spec(col0 + 2)] + page_specs(cache_k)
                  + page_specs(cache_v) + extra_specs),
        out_specs=pl.BlockSpec((t, width), lambda b, s, pt: (b, 0)),
        scratch_shapes=scratch,
    )
    return pl.pallas_call(
        functools.partial(kern, pages_per_step=pages_per_step, n_steps=n_steps),
        grid_spec=grid_spec,
        out_shape=jax.ShapeDtypeStruct((batch * t, width), F32),
        compiler_params=_cparams(("parallel", "arbitrary")),
        name=name,
    )(page_table, pb, pb, pb, *([cache_k] * pages_per_step), *([cache_v] * pages_per_step), *extra_inputs)


def _layer_norm(h, g, b):
    mu = jnp.mean(h, axis=1, keepdims=True)
    d = h - mu
    var = jnp.mean(d * d, axis=1, keepdims=True)
    return d * lax.rsqrt(var + LN_EPS) * g + b


def _post_attn_kernel(x_ref, sb_ref, d_ref, wo_ref, g_ref, b_ref, wr_ref, br_ref, y_ref, idx_ref, gate_ref, *,
                      alpha):
    half = sb_ref.shape[1]
    mix = jnp.dot(sb_ref[...].astype(BF16), wo_ref[:half, :], preferred_element_type=F32)
    mix = mix + jnp.dot(d_ref[...].astype(BF16), wo_ref[half:, :], preferred_element_type=F32)
    y = _layer_norm(alpha * x_ref[...] + mix, g_ref[...], b_ref[...])
    y_ref[...] = y
    logits = jnp.dot(y, wr_ref[...], preferred_element_type=F32, precision=lax.Precision.HIGHEST) + br_ref[...]
    n_exp = logits.shape[1]
    lane = lax.broadcasted_iota(jnp.int32, logits.shape, 1).astype(F32)
    out_lane = lax.broadcasted_iota(jnp.int32, idx_ref.shape, 1)
    work = logits
    idx_out = jnp.zeros(idx_ref.shape, F32)
    val_out = jnp.zeros(gate_ref.shape, F32)
    top = None
    denom = None
    for k in range(TOP_K):
        m = jnp.max(work, axis=1, keepdims=True)
        sel = jnp.min(jnp.where(work == m, lane, float(n_exp)), axis=1, keepdims=True)
        work = jnp.where(lane == sel, -jnp.inf, work)
        if top is None:
            top = m
        e = jnp.exp(m - top)
        denom = e if denom is None else denom + e
        idx_out = jnp.where(out_lane == k, sel, idx_out)
        val_out = jnp.where(out_lane == k, e, val_out)
    idx_ref[...] = idx_out.astype(jnp.int32)
    gate_ref[...] = val_out / denom


def _post_attn(x2d, sb, dd, wo_bf16, g, b, w_router, b_router, tm, alpha):
    n, d = x2d.shape
    half = sb.shape[1]
    full = lambda a: pl.BlockSpec(a.shape, lambda i: (0, 0))
    rows = lambda w: pl.BlockSpec((tm, w), lambda i: (i, 0))
    return pl.pallas_call(
        functools.partial(_post_attn_kernel, alpha=alpha),
        grid=(n // tm,),
        in_specs=[rows(d), rows(half), rows(half), full(wo_bf16), full(g), full(b), full(w_router), full(b_router)],
        out_specs=[rows(d), rows(LANES), rows(LANES)],
        out_shape=[jax.ShapeDtypeStruct((n, d), F32), jax.ShapeDtypeStruct((n, LANES), jnp.int32),
                   jax.ShapeDtypeStruct((n, LANES), F32)],
        compiler_params=_cparams(("parallel",)),
        name="post_attn",
    )(x2d, sb, dd, wo_bf16, g, b, w_router, b_router)


def _row_copy(src_hbm, dst_ref, src_row, dst_row, sem):
    return pltpu.make_async_copy(src_hbm.at[pl.ds(src_row, 1)], dst_ref.at[pl.ds(dst_row, 1)], sem)


def _gather_rows_kernel(tok_ref, src_hbm, o_ref, sem):
    n = o_ref.shape[0]

    def start(r, c):
        _row_copy(src_hbm, o_ref, tok_ref[0, 0, r], r, sem).start()
        return c

    def wait(r, c):
        _row_copy(src_hbm, o_ref, 0, r, sem).wait()
        return c

    lax.fori_loop(0, n, start, 0)
    lax.fori_loop(0, n, wait, 0)


def _gather_rows(src, row_tok, block_rows):
    n_rows = row_tok.shape[0]
    n_blocks = n_rows // block_rows
    d = src.shape[1]
    return pl.pallas_call(
        _gather_rows_kernel,
        grid=(n_blocks,),
        in_specs=[pl.BlockSpec((1, 1, block_rows), lambda j: (j, 0, 0), memory_space=pltpu.SMEM),
                  pl.BlockSpec(memory_space=pl.ANY)],
        out_specs=pl.BlockSpec((block_rows, d), lambda j: (j, 0)),
        out_shape=jax.ShapeDtypeStruct((n_rows, d), src.dtype),
        scratch_shapes=[pltpu.SemaphoreType.DMA(())],
        compiler_params=_cparams(("arbitrary",)),
        name="gather_rows",
    )(row_tok.reshape(n_blocks, 1, block_rows), src)


def _experts_kernel(be_ref, nb_ref, x_ref, gate_ref, wgu_ref, bgu_ref, wd_ref, bd_ref, o_ref):
    del be_ref
    j = pl.program_id(0)
    d_ff = wd_ref.shape[0]

    @pl.when(j < nb_ref[0])
    def _():
        x = x_ref[...].astype(BF16)
        gu = jnp.dot(x, wgu_ref[...], preferred_element_type=F32) + bgu_ref[...]
        g = jnp.minimum(gu[:, :d_ff], SWIGLU_LIMIT)
        u = jnp.clip(gu[:, d_ff:], -SWIGLU_LIMIT, SWIGLU_LIMIT)
        hid = (u + 1.0) * (g * jax.nn.sigmoid(SWIGLU_ALPHA * g))
        out = jnp.dot(hid.astype(BF16), wd_ref[...], preferred_element_type=F32) + bd_ref[...]
        o_ref[...] = out * gate_ref[...]

    @pl.when(j >= nb_ref[0])
    def _():
        o_ref[...] = jnp.zeros_like(o_ref)


def _experts(x_sorted, row_gate, block_expert, n_used, wgu, bgu, wd, bd, block_rows):
    n_rows, d = x_sorted.shape
    n_blocks = n_rows // block_rows
    d_ff = wd.shape[1]
    grid_spec = pltpu.PrefetchScalarGridSpec(
        num_scalar_prefetch=2,
        grid=(n_blocks,),
        in_specs=[pl.BlockSpec((block_rows, d), lambda j, be, nb: (j, 0)),
                  pl.BlockSpec((block_rows, 1), lambda j, be, nb: (j, 0)),
                  pl.BlockSpec((None, d, 2 * d_ff), lambda j, be, nb: (be[j], 0, 0)),
                  pl.BlockSpec((None, 1, 2 * d_ff), lambda j, be, nb: (be[j], 0, 0)),
                  pl.BlockSpec((None, d_ff, d), lambda j, be, nb: (be[j], 0, 0)),
                  pl.BlockSpec((None, 1, d), lambda j, be, nb: (be[j], 0, 0))],
        out_specs=pl.BlockSpec((block_rows, d), lambda j, be, nb: (j, 0)),
    )
    return pl.pallas_call(
        _experts_kernel,
        grid_spec=grid_spec,
        out_shape=jax.ShapeDtypeStruct((n_rows, d), F32),
        compiler_params=_cparams(("arbitrary",)),
        name="experts",
    )(block_expert, n_used, x_sorted, row_gate, wgu, bgu, wd, bd)


def _combine_kernel(dest_ref, y1_ref, eo_hbm, g_ref, b_ref, o_ref, buf, sem, *, alpha):
    tt = y1_ref.shape[0]
    n = TOP_K * tt

    def start(r, c):
        _row_copy(eo_hbm, buf, dest_ref[0, 0, r], r, sem).start()
        return c

    def wait(r, c):
        _row_copy(eo_hbm, buf, 0, r, sem).wait()
        return c

    lax.fori_loop(0, n, start, 0)
    lax.fori_loop(0, n, wait, 0)
    ffn = buf[0:tt, :]
    for k in range(1, TOP_K):
        ffn = ffn + buf[k * tt:(k + 1) * tt, :]
    o_ref[...] = _layer_norm(alpha * y1_ref[...] + ffn, g_ref[...], b_ref[...])


def _combine(y1, expert_out, dest_km, g, b, tt, alpha):
    n, d = y1.shape
    n_tiles = n // tt
    return pl.pallas_call(
        functools.partial(_combine_kernel, alpha=alpha),
        grid=(n_tiles,),
        in_specs=[pl.BlockSpec((1, 1, TOP_K * tt), lambda i: (i, 0, 0), memory_space=pltpu.SMEM),
                  pl.BlockSpec((tt, d), lambda i: (i, 0)),
                  pl.BlockSpec(memory_space=pl.ANY),
                  pl.BlockSpec(g.shape, lambda i: (0, 0)),
                  pl.BlockSpec(b.shape, lambda i: (0, 0))],
        out_specs=pl.BlockSpec((tt, d), lambda i: (i, 0)),
        out_shape=jax.ShapeDtypeStruct((n, d), F32),
        scratch_shapes=[pltpu.VMEM((TOP_K * tt, d), F32), pltpu.SemaphoreType.DMA(())],
        compiler_params=_cparams(("arbitrary",)),
        name="combine",
    )(dest_km, y1, expert_out, g, b)


def _route(top_idx, gates, n_experts, block_rows):
    n = top_idx.shape[0]
    flat_e = top_idx.reshape(-1)
    onehot = (flat_e[:, None] == jnp.arange(n_experts, dtype=jnp.int32)[None, :]).astype(jnp.int32)
    csum = jnp.cumsum(onehot, axis=0)
    pos = jnp.take_along_axis(csum, flat_e[:, None], axis=1)[:, 0] - 1
    counts = csum[-1]
    padded = ((counts + block_rows - 1) // block_rows) * block_rows
    pend = jnp.cumsum(padded)
    pstart = pend - padded
    dest = pstart[flat_e] + pos
    n_rows = -(-(n * TOP_K + n_experts * (block_rows - 1)) // block_rows) * block_rows
    n_blocks = n_rows // block_rows
    flat_tok = jnp.arange(n * TOP_K, dtype=jnp.int32) // TOP_K
    row_tok = jnp.zeros((n_rows,), jnp.int32).at[dest].set(flat_tok)
    row_gate = jnp.zeros((n_rows,), F32).at[dest].set(gates.reshape(-1))
    block_start = jnp.arange(n_blocks, dtype=jnp.int32) * block_rows
    n_used = (pend[-1] // block_rows).astype(jnp.int32)
    block_expert = jnp.clip(jnp.searchsorted(pend, block_start, side="right"), 0, n_experts - 1).astype(jnp.int32)
    last_expert = block_expert[jnp.maximum(n_used - 1, 0)]
    block_expert = jnp.where(jnp.arange(n_blocks) < n_used, block_expert, last_expert)
    return dest.reshape(n, TOP_K).astype(jnp.int32), row_tok, row_gate, block_expert, n_used.reshape(1)


def kernel(x_prompt, x_sample, cache_sb_k, cache_sb_v, cache_diff_k, cache_diff_v, page_table, w_in, lambda_q1, lambda_k1, lambda_q2, lambda_k2, subln_g, w_out, ln1_g, ln1_b, w_router, b_router, w_gate_up, b_gate_up, w_down, b_down, ln2_g, ln2_b):
    depth = w_in.shape[0]
    assert depth == 1, "single-layer step"
    batch, seq, d_model = x_prompt.shape
    dec_batch, dec_seq, _ = x_sample.shape
    n_pool, page = cache_sb_k.shape[1], cache_sb_k.shape[2]
    sb_heads = cache_sb_k.shape[3]
    diff_heads = cache_diff_k.shape[3]
    width = sb_heads * HEAD_DIM
    n_experts = w_router.shape[2]
    lambda_init = 0.8 - 0.6 * math.exp(-0.3 * 0)
    alpha = (2.0 * depth) ** 0.25

    w_in_b = w_in[0].astype(BF16)
    w_out_b = w_out[0].astype(BF16)
    wgu_b = w_gate_up[0].astype(BF16)
    wd_b = w_down[0].astype(BF16)
    lam_params = jnp.stack([lambda_q1[0], lambda_k1[0], lambda_q2[0], lambda_k2[0]])
    g_sub = subln_g[0].reshape(1, -1)
    row = lambda a: a[0].reshape(1, -1)
    slopes = jnp.exp2(-8.0 * jnp.arange(1, diff_heads + 1, dtype=F32) / diff_heads)

    n_p = batch * seq
    xp = x_prompt.reshape(n_p, d_model)
    w = width
    w_nat = jnp.concatenate([w_in_b[:, 0:w], w_in_b[:, 3 * w:4 * w], w_in_b[:, 5 * w:6 * w]], axis=1)
    w_t = jnp.concatenate([w_in_b[:, w:3 * w], w_in_b[:, 4 * w:5 * w]], axis=1).T
    q_b, p_sbk_t, p_sbv_t, p_dk_t, p_dv, kv_t, dv_b = _in_proj_t(x_prompt, w_nat, w_t, min(512, seq))
    sb_p = _sb_prompt(q_b, kv_t)
    d_p = _diff_prompt(q_b, kv_t, dv_b, slopes, lam_params, g_sub, lambda_init)
    y1_p, idx_p, gate_p = _post_attn(xp, sb_p, d_p, w_out_b, row(ln1_g), row(ln1_b), w_router[0], row(b_router),
                                     512, alpha)
    p_sbk = p_sbk_t.reshape(batch, sb_heads, HEAD_DIM, seq).transpose(0, 3, 1, 2)
    p_sbv = p_sbv_t.reshape(batch, sb_heads, HEAD_DIM, seq).transpose(0, 3, 1, 2)
    p_dk = p_dk_t.reshape(batch, diff_heads, 2, HEAD_DIM, seq).transpose(0, 4, 1, 2, 3)

    n_s = dec_batch * dec_seq
    xs = x_sample.reshape(n_s, d_model)
    s_sbk, s_sbv, s_dk, s_dv, pb_s = _in_proj(xs, w_in_b, n_s, F32)
    pages_per_step = 8
    rows_sb = sb_heads * dec_seq
    rows_d = 2 * diff_heads * dec_seq
    sbk_pages = cache_sb_k[0].transpose(0, 2, 3, 1).reshape(n_pool, width, page)
    sbv_pages = cache_sb_v[0].transpose(0, 2, 3, 1).reshape(n_pool, width, page)
    dk_pages = cache_diff_k[0].transpose(0, 2, 3, 4, 1).reshape(n_pool, width, page)
    sb_s = _decode_call(
        _sb_decode_kernel, pb_s, 0, sbk_pages, sbv_pages, page_table, [],
        [pltpu.VMEM((rows_sb, width), F32), pltpu.VMEM((rows_sb, LANES), F32), pltpu.VMEM((rows_sb, width), BF16)],
        dec_seq, pages_per_step, "sb_decode")
    d_s = _decode_call(
        functools.partial(_diff_decode_kernel, n_pages=page_table.shape[1], lambda_init=lambda_init),
        pb_s, 3, dk_pages, cache_diff_v[0], page_table, [lam_params, g_sub],
        [pltpu.VMEM((rows_d, LANES), F32), pltpu.VMEM((rows_d, LANES), F32), pltpu.VMEM((rows_d, LANES), F32),
         pltpu.VMEM((rows_d, width), BF16)],
        dec_seq, pages_per_step, "diff_decode")
    y1_s, idx_s, gate_s = _post_attn(xs, sb_s, d_s, w_out_b, row(ln1_g), row(ln1_b), w_router[0], row(b_router),
                                     n_s, alpha)

    y1 = jnp.concatenate([y1_p, y1_s], axis=0)
    top_idx = jnp.concatenate([idx_p[:, :TOP_K], idx_s[:, :TOP_K]], axis=0)
    gates = jnp.concatenate([gate_p[:, :TOP_K], gate_s[:, :TOP_K]], axis=0)
    n_all = n_p + n_s
    dest, row_tok, row_gate, block_expert, n_used = _route(top_idx, gates, n_experts, EXPERT_ROWS)
    x_sorted = _gather_rows(y1, row_tok, EXPERT_ROWS)
    expert_out = _experts(x_sorted, row_gate.reshape(-1, 1), block_expert, n_used, wgu_b,
                          b_gate_up[0].reshape(n_experts, 1, -1), wd_b, b_down[0].reshape(n_experts, 1, -1),
                          EXPERT_ROWS)
    tt = 128
    dest_km = dest.reshape(n_all // tt, tt, TOP_K).transpose(0, 2, 1).reshape(n_all // tt, 1, TOP_K * tt)
    y = _combine(y1, expert_out, dest_km, row(ln2_g), row(ln2_b), tt, alpha)

    yp = y[:n_p].reshape(batch, seq, d_model)
    ys = y[n_p:].reshape(dec_batch, dec_seq, d_model)
    shp_p = (depth, batch, seq)
    shp_s = (depth, dec_batch, dec_seq)
    return (yp, ys, p_sbk[None], p_sbv[None], p_dk[None], p_dv.reshape(shp_p + (diff_heads, 2 * HEAD_DIM)),
            s_sbk.reshape(shp_s + (sb_heads, HEAD_DIM)), s_sbv.reshape(shp_s + (sb_heads, HEAD_DIM)),
            s_dk.reshape(shp_s + (diff_heads, 2, HEAD_DIM)), s_dv.reshape(shp_s + (diff_heads, 2 * HEAD_DIM)))
```
